```python
import jax
import jax.numpy as jnp
from jax import lax
import numpy as np

D_MODEL = 1024
BATCH = 2
SEQ = 8192
DEPTH = 1
DEC_BATCH = 128
DEC_SEQ = 8
PAST_LEN = 2048
PAGE_SIZE = 128

HG_DK = 128
HG_DV = 128
HG_HEADS = D_MODEL // 128
HG_W = HG_HEADS * HG_DV
HG_CHUNK = 64
NSA_HD = 64
NSA_HEADS = D_MODEL // NSA_HD
NSA_KV = 4
NSA_GROUP = NSA_HEADS // NSA_KV
NSA_W = NSA_HEADS * NSA_HD
KV_W = NSA_KV * NSA_HD
CMP_D = 16
CMP_L = 2 * CMP_D
CMP_HID = 2 * NSA_HD
SLC_L = 64
N_SEL = 16
WINDOW = 512
MEM_TOKENS = 256
MEM_HEADS = 4
MEM_HD = D_MODEL // MEM_HEADS
D_FF = 4 * D_MODEL
ROPE_THETA = 10000.0
LN_EPS = 1e-5
RMS_EPS = 1e-6
DN_ALPHA = (2.0 * DEPTH) ** 0.25
DN_BETA = (8.0 * DEPTH) ** -0.25
ATT_SCALE = NSA_HD ** -0.5
MEM_SCALE = MEM_HD ** -0.5
IN_SIZES = (HG_W, HG_W, HG_W, HG_W, NSA_W, 6 * KV_W, 3 * NSA_HEADS, D_MODEL, D_MODEL)

kernel_name = 'hgrn2_nsa_gated_hybrid_step'


def _pick_block(n, cap):
    b = max(1, min(n, cap))
    while n % b:
        b -= 1
    return b


def _layer_norm(x, g, b):
    xf = x.astype(jnp.float32)
    mu = jnp.mean(xf, axis=-1, keepdims=True)
    var = jnp.mean(jnp.square(xf - mu), axis=-1, keepdims=True)
    return ((xf - mu) * lax.rsqrt(var + LN_EPS) * g.astype(jnp.float32) + b.astype(jnp.float32)).astype(x.dtype)


def _post(x, sub, g, b):
    return _layer_norm(DN_ALPHA * x + sub, g, b)


def _rope(x, pos):
    half = x.shape[-1] // 2
    inv = ROPE_THETA ** (-jnp.arange(half, dtype=jnp.float32) / half)
    ang = jnp.asarray(pos, jnp.float32)[:, None] * inv[None, :]
    cos = jnp.cos(ang)[None, :, None, :]
    sin = jnp.sin(ang)[None, :, None, :]
    xf = x.astype(jnp.float32)
    x1, x2 = xf[..., :half], xf[..., half:]
    return jnp.concatenate([x1 * cos - x2 * sin, x2 * cos + x1 * sin], axis=-1).astype(x.dtype)


def _masked_softmax(s, mask):
    s = jnp.where(mask, s.astype(jnp.float32), -jnp.inf)
    m = jnp.max(s, axis=-1, keepdims=True)
    m = jnp.where(jnp.isfinite(m), m, 0.0)
    e = jnp.exp(s - m)
    return e / jnp.maximum(jnp.sum(e, axis=-1, keepdims=True), 1e-30)


def _split_in(proj):
    cuts = np.cumsum(np.array(IN_SIZES))[:-1].tolist()
    return jnp.split(proj, cuts, axis=-1)


def _gather_pages(pool, page_table):
    g = pool[page_table]
    return g.reshape((page_table.shape[0], page_table.shape[1] * pool.shape[1]) + pool.shape[2:])


def _hgrn2_scan(q, k, v, g, s0):
    B, L, H, DK = q.shape
    DV = v.shape[-1]
    C = _pick_block(L, HG_CHUNK)
    n = L // C

    def chunks(a):
        return a.reshape(B, n, C, H, a.shape[-1]).swapaxes(0, 1)

    causal = jnp.tril(jnp.ones((C, C), dtype=bool))

    def step(S, inp):
        qc, kc, vc, gc = inp
        G = jnp.cumsum(gc, axis=1)
        o_inter = jnp.einsum('bthk,bhkv->bthv', qc * jnp.exp(G), S)
        diff = G[:, :, None] - G[:, None, :]
        decay = jnp.exp(jnp.where(causal[None, :, :, None, None], diff, -jnp.inf))
        A = jnp.einsum('bthk,btshk,bshk->bths', qc, decay, kc)
        o_intra = jnp.einsum('bths,bshv->bthv', A, vc)
        G_last = G[:, -1]
        k_dec = kc * jnp.exp(G_last[:, None] - G)
        S = S * jnp.exp(G_last)[..., None] + jnp.einsum('bshk,bshv->bhkv', k_dec, vc)
        return S, o_inter + o_intra

    S, o = lax.scan(step, s0, (chunks(q), chunks(k), chunks(v), chunks(g)))
    return o.swapaxes(0, 1).reshape(B, L, H, DV), S


def _hgrn2(hq, hf, hi, hog, lb, gnorm, s0):
    B, L, _ = hq.shape
    f32 = jnp.float32
    shp = (B, L, HG_HEADS, HG_DK)
    q = jax.nn.silu(hq.astype(f32)).reshape(shp)
    forget = lb + (1.0 - lb) * jax.nn.sigmoid(hf.astype(f32))
    k = (1.0 - forget).reshape(shp)
    g = jnp.log(forget).reshape(shp)
    v = hi.astype(f32).reshape(B, L, HG_HEADS, HG_DV)
    o, s_new = _hgrn2_scan(q, k, v, g, s0.astype(f32))
    o = o * lax.rsqrt(jnp.mean(o * o, axis=-1, keepdims=True) + RMS_EPS) * gnorm.astype(f32)
    o = o.reshape(B, L, HG_W) * jax.nn.silu(hog.astype(f32))
    return o.astype(hq.dtype), s_new.astype(hq.dtype)


def _compress(rows, pe, w1, w2):
    B, T = rows.shape[:2]
    n_full = T // CMP_D
    ch = rows[:, :n_full * CMP_D].reshape(B, n_full, CMP_D, NSA_KV, NSA_HD)
    first = jnp.einsum('bnlkd,ldh->bnkh', ch, w1[:CMP_D])
    second = jnp.einsum('bnlkd,ldh->bnkh', ch, w1[CMP_D:])
    pos_bias = jnp.einsum('ld,ldh->h', pe, w1)
    return jax.nn.gelu(first[:, :-1] + second[:, 1:] + pos_bias) @ w2


def _nsa_cmp_slc(q, q_rot, cmp_all, slc_all, pos, cmp_pe, cmp_w1, cmp_w2):
    B, L = q.shape[:2]
    T = cmp_all.shape[1]
    kc = _compress(cmp_all[:, :, 0], cmp_pe[0], cmp_w1[0], cmp_w2[0])
    vc = _compress(cmp_all[:, :, 1], cmp_pe[1], cmp_w1[1], cmp_w2[1])
    n_cmp = kc.shape[1]
    n_slc = -(-T // SLC_L)
    n_sel = min(N_SEL, n_slc)
    blk_end = jnp.asarray(np.arange(n_cmp) * CMP_D + CMP_L - 1)
    i0 = np.arange(n_cmp)[:, None] * CMP_D
    j0 = np.arange(n_slc)[None, :] * SLC_L
    inc = jnp.asarray(((i0 < j0 + SLC_L) & (i0 + CMP_L > j0)).astype(np.float32))
    pad = n_slc * SLC_L - T
    slc_p = jnp.pad(slc_all, ((0, 0), (0, pad), (0, 0), (0, 0), (0, 0)))
    kb = slc_p[:, :, 0].reshape(B, n_slc, SLC_L, NSA_KV, NSA_HD).transpose(0, 3, 1, 2, 4)
    vb = slc_p[:, :, 1].reshape(B, n_slc, SLC_L, NSA_KV, NSA_HD).transpose(0, 3, 1, 2, 4)
    QB = _pick_block(L, max(1, 128 // B))
    nq = L // QB

    def blocks(a):
        return a.reshape(B, nq, QB, NSA_KV, NSA_GROUP, NSA_HD).swapaxes(0, 1)

    bi = jnp.arange(B)[:, None, None, None]
    ki = jnp.arange(NSA_KV)[None, :, None, None]
    blk_ids = jnp.arange(n_slc)

    def one(args):
        qc, qr, pb = args
        s_c = jnp.einsum('bqkgd,bnkd->bkgqn', qc, kc) * ATT_SCALE
        pr = _masked_softmax(s_c, blk_end[None, :] <= pb[:, None])
        o_c = jnp.einsum('bkgqn,bnkd->bqkgd', pr.astype(vc.dtype), vc)
        p_slc = jnp.einsum('bkgqn,nj->bkqj', pr, inc)
        cur = pb // SLC_L
        valid = blk_ids[None, :] <= cur[:, None]
        forced = (blk_ids[None, :] == 0) | (blk_ids[None, :] == cur[:, None]) | (blk_ids[None, :] == cur[:, None] - 1)
        score = jnp.where(forced, 1e9, jnp.where(valid, p_slc, -1.0))
        _, idx = lax.top_k(score, n_sel)
        kg = kb[bi, ki, idx]
        vg = vb[bi, ki, idx]
        s_s = jnp.einsum('bqkgd,bkqsld->bkgqsl', qr, kg) * ATT_SCALE
        k_pos = idx[..., None] * SLC_L + jnp.arange(SLC_L)
        m_s = k_pos <= pb[None, None, :, None, None]
        pr_s = _masked_softmax(s_s.reshape(B, NSA_KV, NSA_GROUP, QB, n_sel * SLC_L),
                               m_s.reshape(B, NSA_KV, 1, QB, n_sel * SLC_L))
        o_s = jnp.einsum('bkgqn,bkqnd->bqkgd', pr_s.astype(vg.dtype), vg.reshape(B, NSA_KV, QB, n_sel * SLC_L, NSA_HD))
        return o_c, o_s

    o_c, o_s = lax.map(one, (blocks(q), blocks(q_rot), jnp.asarray(pos, jnp.int32).reshape(nq, QB)))
    o_c = o_c.swapaxes(0, 1).reshape(B, L, NSA_HEADS, NSA_HD)
    o_s = o_s.swapaxes(0, 1).reshape(B, L, NSA_HEADS, NSA_HD)
    return o_c, o_s


def _win_banded(q, kv):
    B, L = q.shape[:2]
    WB = _pick_block(L, 128)
    nb = L // WB
    span = WINDOW + WB
    kvp = jnp.pad(kv, ((0, 0), (WINDOW, 0), (0, 0), (0, 0), (0, 0)))
    qb = q.reshape(B, nb, WB, NSA_KV, NSA_GROUP, NSA_HD).swapaxes(0, 1)

    def one(args):
        qc, start = args
        kvc = lax.dynamic_slice_in_dim(kvp, start, span, axis=1)
        s = jnp.einsum('bqkgd,bskd->bkgqs', qc, kvc[:, :, 0]) * ATT_SCALE
        q_pos = start + jnp.arange(WB)
        k_pos = start - WINDOW + jnp.arange(span)
        mask = (k_pos[None, :] <= q_pos[:, None]) & (k_pos[None, :] > q_pos[:, None] - WINDOW)
        pr = _masked_softmax(s, mask)
        return jnp.einsum('bkgqs,bskd->bqkgd', pr.astype(kvc.dtype), kvc[:, :, 1])

    o = lax.map(one, (qb, jnp.arange(nb, dtype=jnp.int32) * WB))
    return o.swapaxes(0, 1).reshape(B, L, NSA_HEADS, NSA_HD)


def _win_dense(q, kv, q_pos, k_pos):
    B, L = q.shape[:2]
    qg = q.reshape(B, L, NSA_KV, NSA_GROUP, NSA_HD)
    s = jnp.einsum('bqkgd,bskd->bkgqs', qg, kv[:, :, 0]) * ATT_SCALE
    mask = (k_pos[None, :] <= q_pos[:, None]) & (k_pos[None, :] > q_pos[:, None] - WINDOW)
    pr = _masked_softmax(s, jnp.asarray(mask))
    o = jnp.einsum('bkgqs,bskd->bqkgd', pr.astype(kv.dtype), kv[:, :, 1])
    return o.reshape(B, L, NSA_HEADS, NSA_HD)


def _mix(u, pos, lb, hg_s0, cmp_past, slc_past, win_past, w_in, w_out, hg_gnorm, cmp_pe, cmp_w1, cmp_w2):
    B, L, _ = u.shape
    hq, hf, hi, hog, nq, nkv, ngate, gate_a, gate_b = _split_in(u @ w_in)
    o_hg, s_new = _hgrn2(hq, hf, hi, hog, lb, hg_gnorm, hg_s0)
    q = nq.reshape(B, L, NSA_HEADS, NSA_HD)
    q_rot = _rope(q, pos)
    kv = nkv.reshape(B, L, 3, 2, NSA_KV, NSA_HD)
    cmp_new = kv[:, :, 0]
    slc_new = jnp.stack([_rope(kv[:, :, 1, 0], pos), kv[:, :, 1, 1]], axis=2)
    win_new = jnp.stack([_rope(kv[:, :, 2, 0], pos), kv[:, :, 2, 1]], axis=2)
    if cmp_past is None:
        cmp_all, slc_all = cmp_new, slc_new
        o_win = _win_banded(q_rot, win_new)
        win_state = win_new[:, L - min(WINDOW, L):]
    else:
        cmp_all = jnp.concatenate([cmp_past, cmp_new], axis=1)
        slc_all = jnp.concatenate([slc_past, slc_new], axis=1)
        n_buf = win_past.shape[1]
        win_all = jnp.concatenate([win_past, win_new], axis=1)
        k_pos = pos[0] - n_buf + np.arange(n_buf + L)
        o_win = _win_dense(q_rot, win_all, pos, k_pos)
        win_state = win_all[:, L:]
    o_cmp, o_slc = _nsa_cmp_slc(q, q_rot, cmp_all, slc_all, pos, cmp_pe, cmp_w1, cmp_w2)
    g = jax.nn.sigmoid(ngate).reshape(B, L, 3, NSA_HEADS, 1)
    o_nsa = (g[:, :, 0] * o_cmp + g[:, :, 1] * o_slc + g[:, :, 2] * o_win).reshape(B, L, NSA_W).astype(u.dtype)
    merged = jax.nn.sigmoid(gate_a) * o_hg + jax.nn.sigmoid(gate_b) * o_nsa
    return merged @ w_out, (s_new, cmp_new, slc_new, win_state)


def _mem_kv(mem, w_mkv):
    B, M, _ = mem.shape
    return (mem @ w_mkv).reshape(B, M, 2, MEM_HEADS, MEM_HD)


def _mem_attend(x, mkv, w_mq, w_mo):
    B, L, _ = x.shape
    q = (x @ w_mq).reshape(B, L, MEM_HEADS, MEM_HD)
    s = jnp.einsum('bqhd,bmhd->bhqm', q, mkv[:, :, 0]) * MEM_SCALE
    pr = jax.nn.softmax(s.astype(jnp.float32), axis=-1)
    o = jnp.einsum('bhqm,bmhd->bqhd', pr.astype(mkv.dtype), mkv[:, :, 1])
    return o.reshape(B, L, D_MODEL) @ w_mo


def _mlp(x, w_up, w_down):
    h = jax.nn.relu(x @ w_up)
    return (h * h) @ w_down


def setup_inputs(seed: int = 0) -> dict:
    key = jax.random.key(seed)
    ks = jax.random.split(key, 23)
    f32 = jnp.float32
    n_pages = PAST_LEN // PAGE_SIZE
    n_used = DEC_BATCH * n_pages
    n_phys = n_used + (n_used + 3) // 4
    win_buf = min(WINDOW, PAST_LEN)
    n_in = sum(IN_SIZES)

    def nrm(k, shape, scale):
        return jax.random.normal(k, shape, f32) * scale

    page_table = jax.random.permutation(ks[7], n_phys)[:n_used].reshape(DEC_BATCH, n_pages).astype(jnp.int32)
    return {
        'x_prompt': nrm(ks[0], (BATCH, SEQ, D_MODEL), 1.0),
        'x_sample': nrm(ks[1], (DEC_BATCH, DEC_SEQ, D_MODEL), 1.0),
        'state_hgrn': nrm(ks[2], (DEPTH, DEC_BATCH, HG_HEADS, HG_DK, HG_DV), 0.5),
        'cache_cmp_kv': nrm(ks[3], (DEPTH, n_phys, PAGE_SIZE, 2, NSA_KV, NSA_HD), 1.0),
        'cache_slc_kv': nrm(ks[4], (DEPTH, n_phys, PAGE_SIZE, 2, NSA_KV, NSA_HD), 1.0),
        'cache_win_kv': nrm(ks[5], (DEPTH, DEC_BATCH, win_buf, 2, NSA_KV, NSA_HD), 1.0),
        'cache_mem_kv': nrm(ks[6], (DEPTH, DEC_BATCH, MEM_TOKENS, 2, MEM_HEADS, MEM_HD), 1.0),
        'page_table': page_table,
        'mem_prompt': nrm(ks[8], (BATCH, MEM_TOKENS, D_MODEL), 1.0),
        'w_in': nrm(ks[9], (DEPTH, D_MODEL, n_in), D_MODEL ** -0.5),
        'w_out': nrm(ks[10], (DEPTH, D_MODEL, D_MODEL), DN_BETA * D_MODEL ** -0.5),
        'hg_lb': nrm(ks[11], (DEPTH + 1, HG_W), 0.5),
        'hg_gnorm': 1.0 + nrm(ks[12], (DEPTH, HG_DV), 0.02),
        'cmp_pe': nrm(ks[13], (DEPTH, 2, CMP_L, NSA_HD), 0.02),
        'cmp_w1': nrm(ks[14], (DEPTH, 2, CMP_L, NSA_HD, CMP_HID), (CMP_L * NSA_HD) ** -0.5),
        'cmp_w2': nrm(ks[15], (DEPTH, 2, CMP_HID, NSA_HD), CMP_HID ** -0.5),
        'w_mq': nrm(ks[16], (DEPTH, D_MODEL, D_MODEL), D_MODEL ** -0.5),
        'w_mkv': nrm(ks[17], (DEPTH, D_MODEL, 2 * D_MODEL), D_MODEL ** -0.5),
        'w_mo': nrm(ks[18], (DEPTH, D_MODEL, D_MODEL), DN_BETA * D_MODEL ** -0.5),
        'w_up': nrm(ks[19], (DEPTH, D_MODEL, D_FF), D_MODEL ** -0.5),
        'w_down': nrm(ks[20], (DEPTH, D_FF, D_MODEL), DN_BETA * D_FF ** -0.5),
        'ln_g': 1.0 + nrm(ks[21], (DEPTH, 3, D_MODEL), 0.02),
        'ln_b': nrm(ks[22], (DEPTH, 3, D_MODEL), 0.02),
    }


def reference(x_prompt, x_sample, state_hgrn, cache_cmp_kv, cache_slc_kv, cache_win_kv, cache_mem_kv,
              page_table, mem_prompt, w_in, w_out, hg_lb, hg_gnorm, cmp_pe, cmp_w1, cmp_w2,
              w_mq, w_mkv, w_mo, w_up, w_down, ln_g, ln_b):
    pos_p = np.arange(x_prompt.shape[1])
    pos_s = PAST_LEN + np.arange(x_sample.shape[1])
    lb_all = jnp.cumsum(jax.nn.softmax(hg_lb.astype(jnp.float32), axis=0), axis=0)
    xp, xs = x_prompt, x_sample
    B = x_prompt.shape[0]
    hp, cp, sp, wp, mp = [], [], [], [], []
    hs, cs, ss, ws = [], [], [], []
    for l in range(DEPTH):
        mix_p, (h_n, c_n, s_n, w_n) = _mix(xp, pos_p, lb_all[l], jnp.zeros((B, HG_HEADS, HG_DK, HG_DV), jnp.float32),
                                           None, None, None, w_in[l], w_out[l], hg_gnorm[l], cmp_pe[l], cmp_w1[l], cmp_w2[l])
        xp = _post(xp, mix_p, ln_g[l, 0], ln_b[l, 0])
        mkv_p = _mem_kv(mem_prompt, w_mkv[l])
        xp = _post(xp, _mem_attend(xp, mkv_p, w_mq[l], w_mo[l]), ln_g[l, 1], ln_b[l, 1])
        xp = _post(xp, _mlp(xp, w_up[l], w_down[l]), ln_g[l, 2], ln_b[l, 2])
        hp.append(h_n); cp.append(c_n); sp.append(s_n); wp.append(w_n); mp.append(mkv_p)
        cmp_past = _gather_pages(cache_cmp_kv[l], page_table)
        slc_past = _gather_pages(cache_slc_kv[l], page_table)
        mix_s, (h_n, c_n, s_n, w_n) = _mix(xs, pos_s, lb_all[l], state_hgrn[l], cmp_past, slc_past, cache_win_kv[l],
                                           w_in[l], w_out[l], hg_gnorm[l], cmp_pe[l], cmp_w1[l], cmp_w2[l])
        xs = _post(xs, mix_s, ln_g[l, 0], ln_b[l, 0])
        xs = _post(xs, _mem_attend(xs, cache_mem_kv[l], w_mq[l], w_mo[l]), ln_g[l, 1], ln_b[l, 1])
        xs = _post(xs, _mlp(xs, w_up[l], w_down[l]), ln_g[l, 2], ln_b[l, 2])
        hs.append(h_n); cs.append(c_n); ss.append(s_n); ws.append(w_n)
    p_state_hgrn = jnp.stack(hp)
    p_cmp_kv = jnp.stack(cp)
    p_slc_kv = jnp.stack(sp)
    p_win_kv = jnp.stack(wp)
    p_mem_kv = jnp.stack(mp)
    s_state_hgrn = jnp.stack(hs)
    s_cmp_kv = jnp.stack(cs)
    s_slc_kv = jnp.stack(ss)
    s_win_kv = jnp.stack(ws)
    return (xp, xs, p_state_hgrn, p_cmp_kv, p_slc_kv, p_win_kv, p_mem_kv, s_state_hgrn, s_cmp_kv, s_slc_kv, s_win_kv)
```

```python
import functools

import numpy as np
import jax
import jax.numpy as jnp
from jax import lax
from jax.experimental import pallas as pl
from jax.experimental.pallas import tpu as pltpu

F32 = jnp.float32
BF16 = jnp.bfloat16

D_MODEL = 1024
PAST_LEN = 2048
PAGE_SIZE = 128
HG_HEADS = 8
HG_D = 128
NSA_HD = 64
NSA_HEADS = 16
NSA_KV = 4
NSA_GROUP = 4
KV_W = NSA_KV * NSA_HD
CMP_D = 16
CMP_L = 32
CMP_HID = 128
SLC_L = 64
N_SEL = 16
WINDOW = 512
MEM_TOKENS = 256
MEM_HEADS = 4
MEM_HD = 256
D_FF = 4096
ROPE_THETA = 10000.0
LN_EPS = 1e-5
RMS_EPS = 1e-6
DN_ALPHA = 2.0 ** 0.25
ATT_SCALE = NSA_HD ** -0.5
MEM_SCALE = MEM_HD ** -0.5
NEG = -1e30

C_HQ, C_HF, C_HI, C_HOG = 0, 1024, 2048, 3072
C_Q, C_GA, C_GB = 4096, 5120, 6144
C_CMP, C_SLC, C_WIN = 7168, 7680, 8192
P_COLS = 8704

VMEM_LIMIT = 52 * 1024 * 1024


def _params(*sem):
    return pltpu.CompilerParams(dimension_semantics=sem, vmem_limit_bytes=VMEM_LIMIT)


def _dot(a, b):
    return jnp.dot(a, b, preferred_element_type=F32)


def _dot_nt(a, b):
    return lax.dot_general(a, b, (((1,), (1,)), ((), ())), preferred_element_type=F32)


def _dot_tn(a, b):
    return lax.dot_general(a, b, (((0,), (0,)), ((), ())), preferred_element_type=F32)


def _split_bf16(x):
    hi = x.astype(BF16)
    lo = (x - hi.astype(F32)).astype(BF16)
    return hi, lo


def _layer_norm(z, g, b):
    mu = jnp.mean(z, axis=-1, keepdims=True)
    zc = z - mu
    var = jnp.mean(zc * zc, axis=-1, keepdims=True)
    return zc * lax.rsqrt(var + LN_EPS) * g + b


def _mm_kernel(x_ref, w_ref, o_ref):
    o_ref[...] = _dot(x_ref[...].astype(BF16), w_ref[...])


def _mm(x, w, tm, tn, name):
    m, k = x.shape
    n = w.shape[1]
    return pl.pallas_call(
        _mm_kernel,
        grid=(m // tm, n // tn),
        in_specs=[pl.BlockSpec((tm, k), lambda i, j: (i, 0)),
                  pl.BlockSpec((k, tn), lambda i, j: (0, j))],
        out_specs=pl.BlockSpec((tm, tn), lambda i, j: (i, j)),
        out_shape=jax.ShapeDtypeStruct((m, n), F32),
        compiler_params=_params("parallel", "arbitrary"),
        name=name,
    )(x, w)


def _gates_t_kernel(x_ref, wg_ref, o_ref):
    o_ref[...] = jax.nn.sigmoid(_dot_nt(wg_ref[...], x_ref[...].astype(BF16)))


def _gates_t(x, wg_t, tm):
    m, k = x.shape
    n = wg_t.shape[0]
    return pl.pallas_call(
        _gates_t_kernel,
        grid=(m // tm,),
        in_specs=[pl.BlockSpec((tm, k), lambda i: (i, 0)),
                  pl.BlockSpec((n, k), lambda i: (0, 0))],
        out_specs=pl.BlockSpec((n, tm), lambda i: (0, i)),
        out_shape=jax.ShapeDtypeStruct((n, m), F32),
        compiler_params=_params("parallel"),
        name="nsa_gates_t",
    )(x, wg_t)


def _gates_kernel(x_ref, wg_ref, o_ref):
    o_ref[...] = jax.nn.sigmoid(_dot(x_ref[...].astype(BF16), wg_ref[...]))


def _gates(x, wg, tm):
    m, k = x.shape
    n = wg.shape[1]
    return pl.pallas_call(
        _gates_kernel,
        grid=(m // tm,),
        in_specs=[pl.BlockSpec((tm, k), lambda i: (i, 0)),
                  pl.BlockSpec((k, n), lambda i: (0, 0))],
        out_specs=pl.BlockSpec((tm, n), lambda i: (i, 0)),
        out_shape=jax.ShapeDtypeStruct((m, n), F32),
        compiler_params=_params("parallel"),
        name="nsa_gates",
    )(x, wg)


def _mm_ln_kernel(a_ref, w_ref, r_ref, g_ref, b_ref, o_ref):
    y = _dot(a_ref[...].astype(BF16), w_ref[...])
    o_ref[...] = _layer_norm(DN_ALPHA * r_ref[...] + y, g_ref[...], b_ref[...])


def _mm_ln(a, w, resid, g, b, tm, name):
    m, k = a.shape
    n = w.shape[1]
    return pl.pallas_call(
        _mm_ln_kernel,
        grid=(m // tm,),
        in_specs=[pl.BlockSpec((tm, k), lambda i: (i, 0)),
                  pl.BlockSpec((k, n), lambda i: (0, 0)),
                  pl.BlockSpec((tm, n), lambda i: (i, 0)),
                  pl.BlockSpec((1, n), lambda i: (0, 0)),
                  pl.BlockSpec((1, n), lambda i: (0, 0))],
        out_specs=pl.BlockSpec((tm, n), lambda i: (i, 0)),
        out_shape=jax.ShapeDtypeStruct((m, n), F32),
        compiler_params=_params("parallel"),
        name=name,
    )(a, w, resid, g, b)


def _merge_out_ln_kernel(ohg_ref, oc_ref, os_ref, ow_ref, ga_ref, gb_ref, w_ref, r_ref, g_ref, b_ref, o_ref):
    o_nsa = oc_ref[...] + os_ref[...] + ow_ref[...]
    merged = jax.nn.sigmoid(ga_ref[...]) * ohg_ref[...] + jax.nn.sigmoid(gb_ref[...]) * o_nsa
    y = _dot(merged.astype(BF16), w_ref[...])
    o_ref[...] = _layer_norm(DN_ALPHA * r_ref[...] + y, g_ref[...], b_ref[...])


def _merge_out_ln(ohg, oc, os_, ow, proj, w, resid, g, b, tm):
    m, n = resid.shape
    row = pl.BlockSpec((tm, n), lambda i: (i, 0))
    vec = pl.BlockSpec((1, n), lambda i: (0, 0))
    return pl.pallas_call(
        _merge_out_ln_kernel,
        grid=(m // tm,),
        in_specs=[row, row, row, row,
                  pl.BlockSpec((tm, n), lambda i: (i, C_GA // D_MODEL)),
                  pl.BlockSpec((tm, n), lambda i: (i, C_GB // D_MODEL)),
                  pl.BlockSpec((n, n), lambda i: (0, 0)),
                  row, vec, vec],
        out_specs=row,
        out_shape=jax.ShapeDtypeStruct((m, n), F32),
        compiler_params=_params("parallel"),
        name="merge_out_ln",
    )(ohg, oc, os_, ow, proj, proj, w, resid, g, b)


def _mlp_ln_kernel(x_ref, wu_ref, wd_ref, g_ref, b_ref, o_ref, acc_ref):
    j = pl.program_id(1)

    @pl.when(j == 0)
    def _():
        acc_ref[...] = jnp.zeros_like(acc_ref)

    h = jnp.maximum(_dot(x_ref[...].astype(BF16), wu_ref[...]), 0.0)
    acc_ref[...] += _dot((h * h).astype(BF16), wd_ref[...])

    @pl.when(j == pl.num_programs(1) - 1)
    def _():
        o_ref[...] = _layer_norm(DN_ALPHA * x_ref[...] + acc_ref[...], g_ref[...], b_ref[...])


def _mlp_ln(x, wu, wd, g, b, tm, tf):
    m, n = x.shape
    ff = wu.shape[1]
    return pl.pallas_call(
        _mlp_ln_kernel,
        grid=(m // tm, ff // tf),
        in_specs=[pl.BlockSpec((tm, n), lambda i, j: (i, 0)),
                  pl.BlockSpec((n, tf), lambda i, j: (0, j)),
                  pl.BlockSpec((tf, n), lambda i, j: (j, 0)),
                  pl.BlockSpec((1, n), lambda i, j: (0, 0)),
                  pl.BlockSpec((1, n), lambda i, j: (0, 0))],
        out_specs=pl.BlockSpec((tm, n), lambda i, j: (i, 0)),
        out_shape=jax.ShapeDtypeStruct((m, n), F32),
        scratch_shapes=[pltpu.VMEM((tm, n), F32)],
        compiler_params=_params("parallel", "arbitrary"),
        name="mlp_ln",
    )(x, wu, wd, g, b)


def _rope_rows(x, cos, sin):
    w = x.shape[1]
    reps = w // 128
    cos_w = jnp.concatenate([cos] * reps, axis=1) if reps > 1 else cos
    sin_w = jnp.concatenate([sin] * reps, axis=1) if reps > 1 else sin
    lane = lax.broadcasted_iota(jnp.int32, x.shape, 1)
    lower = (lane % NSA_HD) < (NSA_HD // 2)
    partner = jnp.where(lower, pltpu.roll(x, w - NSA_HD // 2, axis=1), pltpu.roll(x, NSA_HD // 2, axis=1))
    return x * cos_w + partner * sin_w


def _rope_split_kernel(q_ref, cmp_ref, slc_ref, win_ref, cos_ref, sin_ref, *outs, want_t):
    cos = cos_ref[...]
    sin = sin_ref[...]
    q = q_ref[...]
    slc = slc_ref[...]
    win = win_ref[...]
    k_slc = _rope_rows(slc[:, :KV_W], cos, sin)
    k_win = _rope_rows(win[:, :KV_W], cos, sin)
    outs[0][...] = q * ATT_SCALE
    outs[1][...] = _rope_rows(q, cos, sin) * ATT_SCALE
    outs[2][...] = cmp_ref[...]
    outs[3][:, :KV_W] = k_slc
    outs[3][:, KV_W:] = slc[:, KV_W:]
    outs[4][:, :KV_W] = k_win
    outs[4][:, KV_W:] = win[:, KV_W:]
    if want_t:
        tm = q.shape[0]
        outs[5][...] = k_slc.astype(BF16)
        outs[6][...] = k_win.astype(BF16)
        outs[7][0] = slc[:, KV_W:].T.astype(BF16)
        v_win_t = win[:, KV_W:].T.astype(BF16)
        for t in range(tm // 128):
            outs[8][t] = v_win_t[:, t * 128:(t + 1) * 128]


def _rope_split(proj, cos, sin, tm, want_t):
    m = proj.shape[0]
    n_tab = cos.shape[0] // tm
    tab = pl.BlockSpec((tm, 128), lambda i: (i % n_tab, 0))
    out_shape = [jax.ShapeDtypeStruct((m, D_MODEL), F32),
                 jax.ShapeDtypeStruct((m, D_MODEL), F32),
                 jax.ShapeDtypeStruct((m, 2 * KV_W), F32),
                 jax.ShapeDtypeStruct((m, 2 * KV_W), F32),
                 jax.ShapeDtypeStruct((m, 2 * KV_W), F32)]
    out_specs = [pl.BlockSpec((tm, D_MODEL), lambda i: (i, 0)),
                 pl.BlockSpec((tm, D_MODEL), lambda i: (i, 0)),
                 pl.BlockSpec((tm, 2 * KV_W), lambda i: (i, 0)),
                 pl.BlockSpec((tm, 2 * KV_W), lambda i: (i, 0)),
                 pl.BlockSpec((tm, 2 * KV_W), lambda i: (i, 0))]
    if want_t:
        out_shape += [jax.ShapeDtypeStruct((m, KV_W), BF16),
                      jax.ShapeDtypeStruct((m, KV_W), BF16),
                      jax.ShapeDtypeStruct((m // tm, KV_W, tm), BF16),
                      jax.ShapeDtypeStruct((m // 128, KV_W, 128), BF16)]
        out_specs += [pl.BlockSpec((tm, KV_W), lambda i: (i, 0)),
                      pl.BlockSpec((tm, KV_W), lambda i: (i, 0)),
                      pl.BlockSpec((1, KV_W, tm), lambda i: (i, 0, 0)),
                      pl.BlockSpec((tm // 128, KV_W, 128), lambda i: (i, 0, 0))]
    return pl.pallas_call(
        functools.partial(_rope_split_kernel, want_t=want_t),
        grid=(m // tm,),
        in_specs=[pl.BlockSpec((tm, D_MODEL), lambda i: (i, C_Q // D_MODEL)),
                  pl.BlockSpec((tm, 2 * KV_W), lambda i: (i, C_CMP // (2 * KV_W))),
                  pl.BlockSpec((tm, 2 * KV_W), lambda i: (i, C_SLC // (2 * KV_W))),
                  pl.BlockSpec((tm, 2 * KV_W), lambda i: (i, C_WIN // (2 * KV_W))),
                  tab, tab],
        out_specs=out_specs,
        out_shape=out_shape,
        compiler_params=_params("parallel"),
        name="rope_split",
    )(proj, proj, proj, proj, cos, sin)


def _cumsum_rows(g):
    c = g.shape[0]
    rowi = lax.broadcasted_iota(jnp.int32, g.shape, 0)
    sh = 1
    while sh < c:
        g = g + jnp.where(rowi >= sh, pltpu.roll(g, sh, axis=0), 0.0)
        sh *= 2
    return g


def _hgrn_head(hq, hf, hi, hog, lb, gn, st, sub, mm_dtype):
    c = hq.shape[0]
    q = jax.nn.silu(hq)
    f = lb + (1.0 - lb) * jax.nn.sigmoid(hf)
    k = 1.0 - f
    gcs = _cumsum_rows(jnp.log(f))
    g_last = gcs[c - 1:c]
    o = _dot_nt((q * jnp.exp(gcs)).astype(mm_dtype), st.astype(mm_dtype))
    rowi = lax.broadcasted_iota(jnp.int32, (sub, HG_D), 0)
    pieces = []
    for i in range(c // sub):
        r0 = i * sub
        gb = gcs[r0:r0 + sub]
        qb = q[r0:r0 + sub]
        kb = k[r0:r0 + sub]
        vb = hi[r0:r0 + sub]
        od = jnp.zeros((sub, HG_D), F32)
        for s in range(sub):
            dec = jnp.exp(jnp.where(rowi >= s, gb - gb[s:s + 1], NEG))
            a = jnp.sum(qb * dec * kb[s:s + 1], axis=1, keepdims=True)
            od = od + a * vb[s:s + 1]
        if i > 0:
            gs = gcs[r0 - 1:r0]
            qt = qb * jnp.exp(gb - gs)
            kt = k[:r0] * jnp.exp(gs - gcs[:r0])
            a = _dot_nt(qt.astype(mm_dtype), kt.astype(mm_dtype))
            od = od + _dot(a.astype(mm_dtype), hi[:r0].astype(mm_dtype))
        pieces.append(od)
    o = o + (jnp.concatenate(pieces, axis=0) if len(pieces) > 1 else pieces[0])
    kd = k * jnp.exp(g_last - gcs)
    st_new = st * jnp.exp(g_last) + _dot_tn(hi.astype(mm_dtype), kd.astype(mm_dtype))
    o = o * lax.rsqrt(jnp.mean(o * o, axis=-1, keepdims=True) + RMS_EPS) * gn
    return o * jax.nn.silu(hog), st_new


def _hgrn_kernel(hq_ref, hf_ref, hi_ref, hog_ref, lbp_ref, gn_ref, *rest, hb, sub, has_s0):
    if has_s0:
        s0_ref, o_ref, sout_ref, st_ref = rest
    else:
        o_ref, sout_ref, st_ref = rest
    c_idx = pl.program_id(2)

    @pl.when(c_idx == 0)
    def _():
        if has_s0:
            for h in range(hb):
                st_ref[h] = s0_ref[0, h].T
        else:
            st_ref[...] = jnp.zeros_like(st_ref)

    a = lbp_ref[...]
    e = jnp.exp(a - jnp.max(a, axis=0, keepdims=True))
    lb_all = e[0:1] / jnp.sum(e, axis=0, keepdims=True)
    mm_dtype = BF16 if hq_ref.shape[0] >= 16 else F32
    for h in range(hb):
        sl = slice(h * HG_D, (h + 1) * HG_D)
        o, st_new = _hgrn_head(hq_ref[:, sl], hf_ref[:, sl], hi_ref[:, sl], hog_ref[:, sl],
                               lb_all[:, sl], gn_ref[...], st_ref[h], sub, mm_dtype)
        o_ref[:, sl] = o
        st_ref[h] = st_new

    @pl.when(c_idx == pl.num_programs(2) - 1)
    def _():
        for h in range(hb):
            sout_ref[0, h] = st_ref[h].T


def _hgrn(proj, hg_lb, gnorm, s0, nb, seq, chunk, hb):
    nc = seq // chunk
    nhb = HG_HEADS // hb
    w = hb * HG_D
    sub = min(16, chunk)

    def col(base):
        return pl.BlockSpec((chunk, w), lambda b, h, c: (b * nc + c, base // w + h))

    in_specs = [col(C_HQ), col(C_HF), col(C_HI), col(C_HOG),
                pl.BlockSpec((hg_lb.shape[0], w), lambda b, h, c: (0, h)),
                pl.BlockSpec((1, HG_D), lambda b, h, c: (0, 0))]
    args = [proj, proj, proj, proj, hg_lb, gnorm]
    if s0 is not None:
        in_specs.append(pl.BlockSpec((1, hb, HG_D, HG_D), lambda b, h, c: (b, h, 0, 0)))
        args.append(s0)
    return pl.pallas_call(
        functools.partial(_hgrn_kernel, hb=hb, sub=sub, has_s0=s0 is not None),
        grid=(nb, nhb, nc),
        in_specs=in_specs,
        out_specs=[pl.BlockSpec((chunk, w), lambda b, h, c: (b * nc + c, h)),
                   pl.BlockSpec((1, hb, HG_D, HG_D), lambda b, h, c: (b, h, 0, 0))],
        out_shape=[jax.ShapeDtypeStruct((nb * seq, HG_HEADS * HG_D), F32),
                   jax.ShapeDtypeStruct((nb, HG_HEADS, HG_D, HG_D), F32)],
        scratch_shapes=[pltpu.VMEM((hb, HG_D, HG_D), F32)],
        compiler_params=_params("parallel", "parallel", "arbitrary"),
        name="hgrn2_scan",
    )(*args)


def _compress_kernel(*refs, n_pages, n_prefetch):
    refs = refs[n_prefetch:]
    page_refs = refs[:n_pages]
    w1_ref, pe_ref, w1f_ref, w2_ref, o_ref, ot_ref = refs[n_pages:]
    pos_bias = jnp.sum(pe_ref[0] * w1f_ref[0], axis=0, keepdims=True)
    accs = [None, None]
    for l in range(CMP_D):
        rows = [p[0, :, l, :] for p in page_refs]
        xl = (jnp.concatenate(rows, axis=0) if n_pages > 1 else rows[0]).astype(BF16)
        for j in range(2):
            t = _dot(xl[:, 128 * j:128 * (j + 1)], w1_ref[0, l])
            accs[j] = t if accs[j] is None else accs[j] + t
    n = accs[0].shape[0]
    hid = []
    for head in range(NSA_KV):
        f = accs[head // 2]
        c0 = (head % 2) * 2 * CMP_HID
        first = f[:, c0:c0 + CMP_HID]
        second = pltpu.roll(f[:, c0 + CMP_HID:c0 + 2 * CMP_HID], n - 1, axis=0)
        hid.append(jax.nn.gelu(first + second + pos_bias))
    out = _dot(jnp.concatenate(hid, axis=1).astype(BF16), w2_ref[0])
    o_ref[0, 0] = out
    ot_ref[0, 0] = out.T


def _compress_weights(cmp_pe, cmp_w1, cmp_w2):
    fs = jnp.concatenate([cmp_w1[:, :CMP_D], cmp_w1[:, CMP_D:]], axis=-1)
    z = jnp.zeros_like(fs)
    w1bd = jnp.concatenate([jnp.concatenate([fs, z], axis=-1),
                            jnp.concatenate([z, fs], axis=-1)], axis=2).astype(BF16)
    pe_col = cmp_pe.reshape(2, CMP_L * NSA_HD, 1)
    w1f = cmp_w1.reshape(2, CMP_L * NSA_HD, CMP_HID)
    w2bd = jnp.zeros((2, NSA_KV, CMP_HID, NSA_KV, NSA_HD), F32)
    for h in range(NSA_KV):
        w2bd = w2bd.at[:, h, :, h, :].set(cmp_w2)
    w2bd = w2bd.reshape(2, NSA_KV * CMP_HID, KV_W).astype(BF16)
    return w1bd, pe_col, w1f, w2bd


def _compress_prompt(cmp_rows, weights, nb, seq):
    n = seq // CMP_D
    x4 = cmp_rows.reshape(nb, n, CMP_D, 2 * KV_W)
    w1bd, pe_col, w1f, w2bd = weights
    return pl.pallas_call(
        functools.partial(_compress_kernel, n_pages=1, n_prefetch=0),
        grid=(nb, 2),
        in_specs=[pl.BlockSpec((1, n, CMP_D, KV_W), lambda b, s: (b, 0, 0, s)),
                  pl.BlockSpec((1, CMP_D, 128, 512), lambda b, s: (s, 0, 0, 0)),
                  pl.BlockSpec((1, CMP_L * NSA_HD, 1), lambda b, s: (s, 0, 0)),
                  pl.BlockSpec((1, CMP_L * NSA_HD, CMP_HID), lambda b, s: (s, 0, 0)),
                  pl.BlockSpec((1, NSA_KV * CMP_HID, KV_W), lambda b, s: (s, 0, 0))],
        out_specs=[pl.BlockSpec((1, 1, n, KV_W), lambda b, s: (b, s, 0, 0)),
                   pl.BlockSpec((1, 1, KV_W, n), lambda b, s: (b, s, 0, 0))],
        out_shape=[jax.ShapeDtypeStruct((nb, 2, n, KV_W), F32),
                   jax.ShapeDtypeStruct((nb, 2, KV_W, n), F32)],
        compiler_params=_params("parallel", "parallel"),
        name="nsa_compress_prompt",
    )(x4, w1bd, pe_col, w1f, w2bd)


def _compress_paged(cache, page_table, weights):
    nseq, n_pages = page_table.shape
    cpp = PAGE_SIZE // CMP_D
    n = n_pages * cpp
    c4 = cache.reshape(cache.shape[0], cpp, CMP_D, 2 * KV_W)
    w1bd, pe_col, w1f, w2bd = weights

    def page_spec(p):
        return pl.BlockSpec((1, cpp, CMP_D, KV_W), lambda b, s, pt: (pt[b, p], 0, 0, s))

    grid_spec = pltpu.PrefetchScalarGridSpec(
        num_scalar_prefetch=1,
        grid=(nseq, 2),
        in_specs=[page_spec(p) for p in range(n_pages)] + [
            pl.BlockSpec((1, CMP_D, 128, 512), lambda b, s, pt: (s, 0, 0, 0)),
            pl.BlockSpec((1, CMP_L * NSA_HD, 1), lambda b, s, pt: (s, 0, 0)),
            pl.BlockSpec((1, CMP_L * NSA_HD, CMP_HID), lambda b, s, pt: (s, 0, 0)),
            pl.BlockSpec((1, NSA_KV * CMP_HID, KV_W), lambda b, s, pt: (s, 0, 0))],
        out_specs=[pl.BlockSpec((1, 1, n, KV_W), lambda b, s, pt: (b, s, 0, 0)),
                   pl.BlockSpec((1, 1, KV_W, n), lambda b, s, pt: (b, s, 0, 0))],
    )
    return pl.pallas_call(
        functools.partial(_compress_kernel, n_pages=n_pages, n_prefetch=1),
        grid_spec=grid_spec,
        out_shape=[jax.ShapeDtypeStruct((nseq, 2, n, KV_W), F32),
                   jax.ShapeDtypeStruct((nseq, 2, KV_W, n), F32)],
        compiler_params=_params("parallel", "parallel"),
        name="nsa_compress_paged",
    )(page_table, *([c4] * n_pages), w1bd, pe_col, w1f, w2bd)


def _place_head(q, head, kv):
    t = q.shape[0]
    parts = []
    if kv > 0:
        parts.append(jnp.zeros((t, NSA_HD * kv), F32))
    parts.append(q[:, head * NSA_HD:(head + 1) * NSA_HD])
    if kv < NSA_KV - 1:
        parts.append(jnp.zeros((t, NSA_HD * (NSA_KV - 1 - kv)), F32))
    return jnp.concatenate(parts, axis=1).astype(BF16)


def _cmp_topk_kernel(q_ref, kc_ref, vct_ref, inct_ref, sg_ref, o_ref, sel_ref, sc_ref, *, tq, n_cmp_pad, n_blk):
    i = pl.program_id(1)
    pos = i * tq + lax.broadcasted_iota(jnp.int32, (1, tq), 1)
    q = q_ref[...]
    kc = kc_ref[0, 0].astype(BF16)
    nrow = lax.broadcasted_iota(jnp.int32, (n_cmp_pad, tq), 0)
    visible = (nrow * CMP_D + (CMP_L - 1)) <= pos
    blk = lax.broadcasted_iota(jnp.int32, (n_blk, tq), 0)
    cur = pos // SLC_L
    valid = blk <= cur
    forced = (blk == 0) | (blk == cur) | (blk == cur - 1)
    inct = inct_ref[...]
    for kv in range(NSA_KV):
        vct = vct_ref[0, 0, kv * NSA_HD:(kv + 1) * NSA_HD, :].astype(BF16)
        prsum = jnp.zeros((n_cmp_pad, tq), F32)
        outs = []
        for h in range(NSA_GROUP):
            head = kv * NSA_GROUP + h
            s = _dot_nt(kc, _place_head(q, head, kv))
            s = jnp.where(visible, s, NEG)
            m = jnp.max(s, axis=0, keepdims=True)
            m = jnp.where(m > 0.5 * NEG, m, 0.0)
            e = jnp.where(visible, jnp.exp(s - m), 0.0)
            pr = e * (1.0 / jnp.maximum(jnp.sum(e, axis=0, keepdims=True), 1e-30))
            prsum = prsum + pr
            outs.append(_dot(vct, pr.astype(BF16)) * sg_ref[head:head + 1, :])
        o_ref[:, kv * KV_W:(kv + 1) * KV_W] = jnp.concatenate(outs, axis=0).T
        hi, lo = _split_bf16(prsum)
        p_slc = _dot(inct, hi) + _dot(inct, lo)
        score = jnp.where(forced, 1e9, jnp.where(valid, p_slc, -1.0))
        sc_ref[...] = score

        def body(r, cnt):
            row = sc_ref[pl.ds(r, 1), :]
            ahead = jnp.where(blk > r, (row >= score).astype(F32), (row > score).astype(F32))
            return cnt + ahead

        cnt = lax.fori_loop(0, n_blk, body, jnp.zeros((n_blk, tq), F32))
        sel_ref[0, kv] = jnp.where(valid & (cnt < float(N_SEL)), 1.0, 0.0)


def _cmp_topk(qs, kcvc, kcvc_t, inct, sg_t, nb, seq, tq):
    nq = seq // tq
    n_cmp_pad = kcvc.shape[2]
    n_blk = inct.shape[0]
    return pl.pallas_call(
        functools.partial(_cmp_topk_kernel, tq=tq, n_cmp_pad=n_cmp_pad, n_blk=n_blk),
        grid=(nb, nq),
        in_specs=[pl.BlockSpec((tq, D_MODEL), lambda b, i: (b * nq + i, 0)),
                  pl.BlockSpec((1, 1, n_cmp_pad, KV_W), lambda b, i: (b, 0, 0, 0)),
                  pl.BlockSpec((1, 1, KV_W, n_cmp_pad), lambda b, i: (b, 1, 0, 0)),
                  pl.BlockSpec((n_blk, n_cmp_pad), lambda b, i: (0, 0)),
                  pl.BlockSpec((3 * NSA_HEADS, tq), lambda b, i: (0, b * nq + i))],
        out_specs=[pl.BlockSpec((tq, D_MODEL), lambda b, i: (b * nq + i, 0)),
                   pl.BlockSpec((1, NSA_KV, n_blk, tq), lambda b, i: (b, 0, 0, i))],
        out_shape=[jax.ShapeDtypeStruct((nb * seq, D_MODEL), F32),
                   jax.ShapeDtypeStruct((nb, NSA_KV, n_blk, seq), F32)],
        scratch_shapes=[pltpu.VMEM((n_blk, tq), F32)],
        compiler_params=_params("parallel", "parallel"),
        name="nsa_cmp_topk",
    )(qs, kcvc, kcvc_t, inct, sg_t)


def _flash_t_step(s, mask, vt, m, l, acc):
    s = jnp.where(mask, s, NEG)
    m_new = jnp.maximum(m, jnp.max(s, axis=0, keepdims=True))
    alpha = jnp.exp(m - m_new)
    p = jnp.where(mask, jnp.exp(s - m_new), 0.0)
    l_new = alpha * l + jnp.sum(p, axis=0, keepdims=True)
    acc_new = alpha * acc + _dot(vt, p.astype(BF16))
    return m_new, l_new, acc_new


def _slc_attn_kernel(q_ref, k_ref, vt_ref, sel_ref, sg_ref, o_ref, *, tq, tk):
    i = pl.program_id(1)
    q = q_ref[...]
    n = NSA_GROUP * tq
    qpos = i * tq + lax.broadcasted_iota(jnp.int32, (tk, n), 1) % tq
    krow = lax.broadcasted_iota(jnp.int32, (tk, n), 0)
    n_tiles = (i * tq + tq - 1) // tk + 1
    bpt = tk // SLC_L
    for kv in range(NSA_KV):
        qp = jnp.concatenate([_place_head(q, kv * NSA_GROUP + h, kv) for h in range(NSA_GROUP)], axis=0)

        def body(j, carry, kv=kv, qp=qp):
            m, l, acc = carry
            start = pl.multiple_of(j * tk, tk)
            s = _dot_nt(k_ref[pl.ds(start, tk), :], qp)
            selrows = sel_ref[0, kv, pl.ds(pl.multiple_of(j * bpt, bpt), bpt), :]
            selrows = jnp.concatenate([selrows] * NSA_GROUP, axis=1)
            selx = jnp.concatenate([jnp.broadcast_to(selrows[r:r + 1], (SLC_L, n)) for r in range(bpt)], axis=0)
            mask = (selx > 0.5) & ((start + krow) <= qpos)
            vt = vt_ref[j, kv * NSA_HD:(kv + 1) * NSA_HD, :]
            return _flash_t_step(s, mask, vt, m, l, acc)

        m0 = jnp.full((1, n), NEG, F32)
        l0 = jnp.zeros((1, n), F32)
        a0 = jnp.zeros((NSA_HD, n), F32)
        m, l, acc = lax.fori_loop(0, n_tiles, body, (m0, l0, a0))
        o = acc * (1.0 / jnp.maximum(l, 1e-30))
        outs = []
        for h in range(NSA_GROUP):
            head = kv * NSA_GROUP + h
            outs.append(o[:, h * tq:(h + 1) * tq] * sg_ref[NSA_HEADS + head:NSA_HEADS + head + 1, :])
        o_ref[:, kv * KV_W:(kv + 1) * KV_W] = jnp.concatenate(outs, axis=0).T


def _slc_attn(qrs, k_rows, v_t, sel_t, sg_t, nb, seq, tq, tk):
    nq = seq // tq
    nkt = seq // tk
    n_blk = sel_t.shape[2]
    return pl.pallas_call(
        functools.partial(_slc_attn_kernel, tq=tq, tk=tk),
        grid=(nb, nq),
        in_specs=[pl.BlockSpec((tq, D_MODEL), lambda b, i: (b * nq + i, 0)),
                  pl.BlockSpec((seq, KV_W), lambda b, i: (b, 0)),
                  pl.BlockSpec((nkt, KV_W, tk), lambda b, i: (b, 0, 0)),
                  pl.BlockSpec((1, NSA_KV, n_blk, tq), lambda b, i: (b, 0, 0, i)),
                  pl.BlockSpec((3 * NSA_HEADS, tq), lambda b, i: (0, b * nq + i))],
        out_specs=pl.BlockSpec((tq, D_MODEL), lambda b, i: (b * nq + i, 0)),
        out_shape=jax.ShapeDtypeStruct((nb * seq, D_MODEL), F32),
        compiler_params=_params("parallel", "parallel"),
        name="nsa_slc_attn",
    )(qrs, k_rows, v_t, sel_t, sg_t)


def _win_attn_kernel(q_ref, k_ref, vt_ref, sg_ref, o_ref, *, tq):
    i = pl.program_id(1)
    q = q_ref[...]
    n = NSA_GROUP * tq
    qpos = i * tq + lax.broadcasted_iota(jnp.int32, (tq, n), 1) % tq
    krow = lax.broadcasted_iota(jnp.int32, (tq, n), 0)
    n_back = WINDOW // tq
    qpos1 = i * tq + lax.broadcasted_iota(jnp.int32, (1, tq), 1)
    n_pad = jnp.maximum(WINDOW - 1 - qpos1, 0).astype(F32)
    n_pad = jnp.concatenate([n_pad] * NSA_GROUP, axis=1)
    for kv in range(NSA_KV):
        qp = jnp.concatenate([_place_head(q, kv * NSA_GROUP + h, kv) for h in range(NSA_GROUP)], axis=0)
        m = jnp.full((1, n), NEG, F32)
        l = jnp.zeros((1, n), F32)
        acc = jnp.zeros((NSA_HD, n), F32)
        for t in range(n_back + 1):
            jt = i - n_back + t
            jc = jnp.maximum(jt, 0)
            start = pl.multiple_of(jc * tq, tq)
            s = _dot_nt(k_ref[pl.ds(start, tq), :], qp)
            kpos = start + krow
            mask = (kpos <= qpos) & (kpos > qpos - WINDOW) & (jt >= 0)
            vt = vt_ref[jc, kv * NSA_HD:(kv + 1) * NSA_HD, :]
            m, l, acc = _flash_t_step(s, mask, vt, m, l, acc)
        m2 = jnp.where(n_pad > 0.0, jnp.maximum(m, 0.0), m)
        alpha = jnp.exp(m - m2)
        l = alpha * l + n_pad * jnp.exp(-m2)
        o = (alpha * acc) * (1.0 / jnp.maximum(l, 1e-30))
        outs = []
        for h in range(NSA_GROUP):
            head = kv * NSA_GROUP + h
            outs.append(o[:, h * tq:(h + 1) * tq] * sg_ref[2 * NSA_HEADS + head:2 * NSA_HEADS + head + 1, :])
        o_ref[:, kv * KV_W:(kv + 1) * KV_W] = jnp.concatenate(outs, axis=0).T


def _win_attn(qrs, k_rows, v_t, sg_t, nb, seq, tq):
    nq = seq // tq
    return pl.pallas_call(
        functools.partial(_win_attn_kernel, tq=tq),
        grid=(nb, nq),
        in_specs=[pl.BlockSpec((tq, D_MODEL), lambda b, i: (b * nq + i, 0)),
                  pl.BlockSpec((seq, KV_W), lambda b, i: (b, 0)),
                  pl.BlockSpec((nq, KV_W, tq), lambda b, i: (b, 0, 0)),
                  pl.BlockSpec((3 * NSA_HEADS, tq), lambda b, i: (0, b * nq + i))],
        out_specs=pl.BlockSpec((tq, D_MODEL), lambda b, i: (b * nq + i, 0)),
        out_shape=jax.ShapeDtypeStruct((nb * seq, D_MODEL), F32),
        compiler_params=_params("parallel", "parallel"),
        name="nsa_win_attn",
    )(qrs, k_rows, v_t, sg_t)


def _flash_step(s, mask, v, m, l, acc):
    s = jnp.where(mask, s, NEG)
    m_new = jnp.maximum(m, jnp.max(s, axis=1, keepdims=True))
    alpha = jnp.exp(m - m_new)
    p = jnp.where(mask, jnp.exp(s - m_new), 0.0)
    l_new = alpha * l + jnp.sum(p, axis=1, keepdims=True)
    acc_new = alpha * acc + _dot(p.astype(BF16), v)
    return m_new, l_new, acc_new


def _stack_heads(q):
    return jnp.concatenate([_place_head(q, kv * NSA_GROUP + h, kv)
                            for kv in range(NSA_KV) for h in range(NSA_GROUP)], axis=0)


def _unstack_heads(o, lq):
    return jnp.concatenate([o[(kv * NSA_GROUP + h) * lq:(kv * NSA_GROUP + h + 1) * lq, kv * NSA_HD:(kv + 1) * NSA_HD]
                            for kv in range(NSA_KV) for h in range(NSA_GROUP)], axis=1)


def _expand_gate(sg, e_ref):
    hi, lo = _split_bf16(sg)
    return _dot(hi, e_ref[...]) + _dot(lo, e_ref[...])


def _nsa_sample_kernel(*refs, n_pages, lq, past, n_buf, n_cmp, n_slc):
    pt_ref = refs[0]
    del pt_ref
    page_refs = refs[1:1 + n_pages]
    (q_ref, qr_ref, slc_new_ref, win_new_ref, win_past_ref, kcvc_ref, inc_ref, eg_ref, sg_ref,
     o_ref, win_out_ref) = refs[1 + n_pages:]
    rows = NSA_HEADS * lq
    tile = PAGE_SIZE
    qpos_col = past + (lax.broadcasted_iota(jnp.int32, (rows, 1), 0) % lq)
    lane = lax.broadcasted_iota(jnp.int32, (rows, tile), 1)
    sg = sg_ref[...]

    qs = _stack_heads(q_ref[...])
    kc = kcvc_ref[0, 0].astype(BF16)
    vc = kcvc_ref[0, 1].astype(BF16)
    n_cmp_pad = kc.shape[0]
    lane_c = lax.broadcasted_iota(jnp.int32, (rows, n_cmp_pad), 1)
    vis = (lane_c < n_cmp) & ((lane_c * CMP_D + (CMP_L - 1)) <= qpos_col)
    s = jnp.where(vis, _dot_nt(qs, kc), NEG)
    m = jnp.max(s, axis=1, keepdims=True)
    m = jnp.where(m > 0.5 * NEG, m, 0.0)
    e = jnp.where(vis, jnp.exp(s - m), 0.0)
    pr = e * (1.0 / jnp.maximum(jnp.sum(e, axis=1, keepdims=True), 1e-30))
    o_cmp = _dot(pr.astype(BF16), vc)

    grp = NSA_GROUP * lq
    prsum = jnp.concatenate(
        [sum(pr[kv * grp + h * lq:kv * grp + (h + 1) * lq] for h in range(NSA_GROUP)) for kv in range(NSA_KV)], axis=0)
    hi, lo = _split_bf16(prsum)
    p_slc = _dot(hi, inc_ref[...]) + _dot(lo, inc_ref[...])
    n_blk_pad = p_slc.shape[1]
    blk = lax.broadcasted_iota(jnp.int32, (NSA_KV * lq, n_blk_pad), 1)
    cur = (past + (lax.broadcasted_iota(jnp.int32, (NSA_KV * lq, 1), 0) % lq)) // SLC_L
    valid = (blk <= cur) & (blk < n_slc)
    forced = (blk == 0) | (blk == cur) | (blk == cur - 1)
    score = jnp.where(forced, 1e9, jnp.where(valid, p_slc, -1.0))
    cnt = jnp.zeros(score.shape, F32)
    for r in range(n_slc):
        colv = score[:, r:r + 1]
        cnt = cnt + jnp.where(blk > r, (colv >= score).astype(F32), (colv > score).astype(F32))
    sel = jnp.where(valid & (cnt < float(min(N_SEL, n_slc))), 1.0, 0.0)
    sel_rows = jnp.concatenate([sel[kv * lq:(kv + 1) * lq] for kv in range(NSA_KV) for _ in range(NSA_GROUP)], axis=0)

    qr = _stack_heads(qr_ref[...])
    m = jnp.full((rows, 1), NEG, F32)
    l = jnp.zeros((rows, 1), F32)
    acc = jnp.zeros((rows, KV_W), F32)
    bpp = tile // SLC_L
    for p in range(n_pages + 1):
        if p < n_pages:
            page = page_refs[p][0]
        else:
            page = jnp.concatenate([slc_new_ref[...], jnp.zeros((tile - lq, 2 * KV_W), F32)], axis=0)
        kpos = p * tile + lane
        selx = sel_rows[:, p * bpp:p * bpp + 1]
        for r in range(1, bpp):
            selx = jnp.where(lane >= r * SLC_L, sel_rows[:, p * bpp + r:p * bpp + r + 1], selx)
        mask = (selx > 0.5) & (kpos <= qpos_col) & (kpos < past + lq)
        s = _dot_nt(qr, page[:, :KV_W].astype(BF16))
        m, l, acc = _flash_step(s, mask, page[:, KV_W:].astype(BF16), m, l, acc)
    o_slc = acc * (1.0 / jnp.maximum(l, 1e-30))

    m = jnp.full((rows, 1), NEG, F32)
    l = jnp.zeros((rows, 1), F32)
    acc = jnp.zeros((rows, KV_W), F32)
    n_wt = n_buf // tile
    for t in range(n_wt + 1):
        if t < n_wt:
            blkrows = win_past_ref[0, t * tile:(t + 1) * tile, :]
            kpos = past - n_buf + t * tile + lane
            ok = kpos >= 0
        else:
            blkrows = jnp.concatenate([win_new_ref[...], jnp.zeros((tile - lq, 2 * KV_W), F32)], axis=0)
            kpos = past + lane
            ok = lane < lq
        mask = ok & (kpos <= qpos_col) & (kpos > qpos_col - WINDOW)
        s = _dot_nt(qr, blkrows[:, :KV_W].astype(BF16))
        m, l, acc = _flash_step(s, mask, blkrows[:, KV_W:].astype(BF16), m, l, acc)
    o_win = acc * (1.0 / jnp.maximum(l, 1e-30))

    o_ref[...] = (_expand_gate(sg[:, 0:NSA_HEADS], eg_ref) * _unstack_heads(o_cmp, lq)
                  + _expand_gate(sg[:, NSA_HEADS:2 * NSA_HEADS], eg_ref) * _unstack_heads(o_slc, lq)
                  + _expand_gate(sg[:, 2 * NSA_HEADS:], eg_ref) * _unstack_heads(o_win, lq))
    win_out_ref[0, :n_buf - lq, :] = win_past_ref[0, lq:, :]
    win_out_ref[0, n_buf - lq:, :] = win_new_ref[...]


def _nsa_sample(qs, qrs, slc_new, win_new, cache_slc, cache_win, page_table, kcvc, inc_pad, e_gate, sg, lq):
    nseq, n_pages = page_table.shape
    n_buf = cache_win.shape[1]
    past = n_pages * PAGE_SIZE
    n_cmp = past // CMP_D - 1
    n_slc = -(-(past + lq) // SLC_L)
    n_cmp_pad = kcvc.shape[2]

    def page_spec(p):
        return pl.BlockSpec((1, PAGE_SIZE, 2 * KV_W), lambda b, pt: (pt[b, p], 0, 0))

    grid_spec = pltpu.PrefetchScalarGridSpec(
        num_scalar_prefetch=1,
        grid=(nseq,),
        in_specs=[page_spec(p) for p in range(n_pages)] + [
            pl.BlockSpec((lq, D_MODEL), lambda b, pt: (b, 0)),
            pl.BlockSpec((lq, D_MODEL), lambda b, pt: (b, 0)),
            pl.BlockSpec((lq, 2 * KV_W), lambda b, pt: (b, 0)),
            pl.BlockSpec((lq, 2 * KV_W), lambda b, pt: (b, 0)),
            pl.BlockSpec((1, n_buf, 2 * KV_W), lambda b, pt: (b, 0, 0)),
            pl.BlockSpec((1, 2, n_cmp_pad, KV_W), lambda b, pt: (b, 0, 0, 0)),
            pl.BlockSpec(inc_pad.shape, lambda b, pt: (0, 0)),
            pl.BlockSpec(e_gate.shape, lambda b, pt: (0, 0)),
            pl.BlockSpec((lq, 3 * NSA_HEADS), lambda b, pt: (b, 0))],
        out_specs=[pl.BlockSpec((lq, D_MODEL), lambda b, pt: (b, 0)),
                   pl.BlockSpec((1, n_buf, 2 * KV_W), lambda b, pt: (b, 0, 0))],
    )
    return pl.pallas_call(
        functools.partial(_nsa_sample_kernel, n_pages=n_pages, lq=lq, past=past, n_buf=n_buf, n_cmp=n_cmp, n_slc=n_slc),
        grid_spec=grid_spec,
        out_shape=[jax.ShapeDtypeStruct((nseq * lq, D_MODEL), F32),
                   jax.ShapeDtypeStruct((nseq, n_buf, 2 * KV_W), F32)],
        compiler_params=_params("parallel"),
        name="nsa_sample",
    )(page_table, *([cache_slc] * n_pages), qs, qrs, slc_new, win_new, cache_win, kcvc, inc_pad, e_gate, sg)


def _mem_attn_kernel(q_ref, k_ref, v_ref, o_ref):
    q = (q_ref[...] * MEM_SCALE).astype(BF16)
    k = k_ref[0].astype(BF16)
    v = v_ref[0].astype(BF16)
    outs = []
    for h in range(MEM_HEADS):
        sl = slice(h * MEM_HD, (h + 1) * MEM_HD)
        s = _dot_nt(q[:, sl], k[:, sl])
        e = jnp.exp(s - jnp.max(s, axis=1, keepdims=True))
        pr = e * (1.0 / jnp.sum(e, axis=1, keepdims=True))
        outs.append(_dot(pr.astype(BF16), v[:, sl]))
    o_ref[...] = jnp.concatenate(outs, axis=1)


def _mem_attn(q, mkv, nb, seq, tq):
    nq = seq // tq
    return pl.pallas_call(
        _mem_attn_kernel,
        grid=(nb, nq),
        in_specs=[pl.BlockSpec((tq, D_MODEL), lambda b, i: (b * nq + i, 0)),
                  pl.BlockSpec((1, MEM_TOKENS, D_MODEL), lambda b, i: (b, 0, 0)),
                  pl.BlockSpec((1, MEM_TOKENS, D_MODEL), lambda b, i: (b, 0, 1))],
        out_specs=pl.BlockSpec((tq, D_MODEL), lambda b, i: (b * nq + i, 0)),
        out_shape=jax.ShapeDtypeStruct((nb * seq, D_MODEL), F32),
        compiler_params=_params("parallel", "parallel"),
        name="mem_attn",
    )(q, mkv, mkv)


def _rope_tables(pos):
    half = NSA_HD // 2
    inv = ROPE_THETA ** (-jnp.arange(half, dtype=F32) / half)
    ang = jnp.asarray(pos, F32)[:, None] * inv[None, :]
    cos = jnp.cos(ang)
    sin = jnp.sin(ang)
    cos128 = jnp.concatenate([cos, cos, cos, cos], axis=1)
    sin128 = jnp.concatenate([-sin, sin, -sin, sin], axis=1)
    return cos128, sin128


def _overlap(n_cmp, n_slc):
    i0 = np.arange(n_cmp)[:, None] * CMP_D
    j0 = np.arange(n_slc)[None, :] * SLC_L
    return ((i0 < j0 + SLC_L) & (i0 + CMP_L > j0)).astype(np.float32)


def kernel(x_prompt, x_sample, state_hgrn, cache_cmp_kv, cache_slc_kv, cache_win_kv, cache_mem_kv, page_table,
           mem_prompt, w_in, w_out, hg_lb, hg_gnorm, cmp_pe, cmp_w1, cmp_w2, w_mq, w_mkv, w_mo, w_up, w_down,
           ln_g, ln_b):
    nb, seq, _ = x_prompt.shape
    ns, lq, _ = x_sample.shape
    depth = w_in.shape[0]
    assert depth == 1
    n_phys = cache_cmp_kv.shape[1]
    n_buf = cache_win_kv.shape[2]

    w = w_in[0]
    proj_w = jnp.concatenate([w[:, 0:5120], w[:, 6704:8752], w[:, 5120:6656]], axis=1).astype(BF16)
    wg = w[:, 6656:6704].astype(BF16)
    wg_t = wg.T
    w_out_b = w_out[0].astype(BF16)
    w_mq_b = w_mq[0].astype(BF16)
    w_mkv_b = w_mkv[0].astype(BF16)
    w_mo_b = w_mo[0].astype(BF16)
    w_up_b = w_up[0].astype(BF16)
    w_down_b = w_down[0].astype(BF16)
    gnorm = hg_gnorm[0:1]
    lng = [ln_g[0, i:i + 1] for i in range(3)]
    lnb = [ln_b[0, i:i + 1] for i in range(3)]
    cweights = _compress_weights(cmp_pe[0], cmp_w1[0], cmp_w2[0])

    xp = x_prompt.reshape(nb * seq, D_MODEL)
    proj_p = _mm(xp, proj_w, 1024, 512, "in_proj_prompt")
    sg_t = _gates_t(xp, wg_t, 512)
    cos_p, sin_p = _rope_tables(np.arange(seq))
    tk = 512
    (qs_p, qrs_p, cmp_p, slc_p, win_p, kslc_b, kwin_b, vslc_t, vwin_t) = _rope_split(proj_p, cos_p, sin_p, tk, True)
    ohg_p, state_p = _hgrn(proj_p, hg_lb, gnorm, None, nb, seq, 64, 2)
    kcvc_p, kcvc_pt = _compress_prompt(cmp_p, cweights, nb, seq)
    n_cmp_pad = seq // CMP_D
    n_blk = seq // SLC_L
    inct = jnp.asarray(_overlap(n_cmp_pad, n_blk).T, BF16)
    tq = 128
    oc_p, sel_t = _cmp_topk(qs_p, kcvc_p, kcvc_pt, inct, sg_t, nb, seq, tq)
    os_p = _slc_attn(qrs_p, kslc_b, vslc_t, sel_t, sg_t, nb, seq, tq, tk)
    ow_p = _win_attn(qrs_p, kwin_b, vwin_t, sg_t, nb, seq, tq)
    x1_p = _merge_out_ln(ohg_p, oc_p, os_p, ow_p, proj_p, w_out_b, xp, lng[0], lnb[0], 256)
    mkv_p = _mm(mem_prompt.reshape(nb * MEM_TOKENS, D_MODEL), w_mkv_b, 256, 512, "mem_kv_proj")
    mkv_p3 = mkv_p.reshape(nb, MEM_TOKENS, 2 * D_MODEL)
    qm_p = _mm(x1_p, w_mq_b, 1024, 512, "mem_q_prompt")
    om_p = _mem_attn(qm_p, mkv_p3, nb, seq, 256)
    x2_p = _mm_ln(om_p, w_mo_b, x1_p, lng[1], lnb[1], 512, "mem_out_ln_prompt")
    x3_p = _mlp_ln(x2_p, w_up_b, w_down_b, lng[2], lnb[2], 1024, 512)

    xs = x_sample.reshape(ns * lq, D_MODEL)
    ms = ns * lq
    proj_s = _mm(xs, proj_w, ms, 512, "in_proj_sample")
    sg_s = _gates(xs, wg, ms)
    cos_s, sin_s = _rope_tables(PAST_LEN + np.arange(lq))
    cos_s = jnp.tile(cos_s, (ns, 1))
    sin_s = jnp.tile(sin_s, (ns, 1))
    qs_s, qrs_s, cmp_s, slc_s, win_s = _rope_split(proj_s, cos_s, sin_s, ms, False)
    ohg_s, state_s = _hgrn(proj_s, hg_lb, gnorm, state_hgrn[0], ns, lq, lq, HG_HEADS)
    cache_cmp = cache_cmp_kv[0].reshape(n_phys, PAGE_SIZE, 2 * KV_W)
    cache_slc = cache_slc_kv[0].reshape(n_phys, PAGE_SIZE, 2 * KV_W)
    cache_win = cache_win_kv[0].reshape(ns, n_buf, 2 * KV_W)
    kcvc_s, _ = _compress_paged(cache_cmp, page_table, cweights)
    n_cmp_s = PAST_LEN // CMP_D - 1
    n_slc_s = -(-(PAST_LEN + lq) // SLC_L)
    inc_np = np.zeros((kcvc_s.shape[2], 128), np.float32)
    inc_np[:n_cmp_s, :n_slc_s] = _overlap(n_cmp_s, n_slc_s)
    e_np = np.zeros((NSA_HEADS, D_MODEL), np.float32)
    for h in range(NSA_HEADS):
        e_np[h, h * NSA_HD:(h + 1) * NSA_HD] = 1.0
    onsa_s, win_state_s = _nsa_sample(qs_s, qrs_s, slc_s, win_s, cache_slc, cache_win, page_table, kcvc_s,
                                      jnp.asarray(inc_np, BF16), jnp.asarray(e_np, BF16), sg_s, lq)
    zeros_s = jnp.zeros_like(onsa_s)
    x1_s = _merge_out_ln(ohg_s, onsa_s, zeros_s, zeros_s, proj_s, w_out_b, xs, lng[0], lnb[0], 256)
    qm_s = _mm(x1_s, w_mq_b, ms, 512, "mem_q_sample")
    mem_s = cache_mem_kv[0].reshape(ns, MEM_TOKENS, 2 * D_MODEL)
    om_s = _mem_attn(qm_s, mem_s, ns, lq, lq)
    x2_s = _mm_ln(om_s, w_mo_b, x1_s, lng[1], lnb[1], 512, "mem_out_ln_sample")
    x3_s = _mlp_ln(x2_s, w_up_b, w_down_b, lng[2], lnb[2], ms, 512)

    kvshape = (2, NSA_KV, NSA_HD)
    n_win = min(WINDOW, seq)
    y_prompt = x3_p.reshape(nb, seq, D_MODEL)
    y_sample = x3_s.reshape(ns, lq, D_MODEL)
    p_state = state_p[None]
    p_cmp = cmp_p.reshape((1, nb, seq) + kvshape)
    p_slc = slc_p.reshape((1, nb, seq) + kvshape)
    p_win = win_p.reshape((nb, seq) + kvshape)[None, :, seq - n_win:]
    p_mem = mkv_p.reshape(1, nb, MEM_TOKENS, 2, MEM_HEADS, MEM_HD)
    s_state = state_s[None]
    s_cmp = cmp_s.reshape((1, ns, lq) + kvshape)
    s_slc = slc_s.reshape((1, ns, lq) + kvshape)
    s_win = win_state_s.reshape((1, ns, n_buf) + kvshape)
    return (y_prompt, y_sample, p_state, p_cmp, p_slc, p_win, p_mem, s_state, s_cmp, s_slc, s_win)
```
